```python
import math
import jax, jax.numpy as jnp
from jax import lax
import numpy as np

D_MODEL = 1024
BATCH = 1
SEQ = 16384
DEPTH = 2

N_EVEN = (DEPTH + 1) // 2
N_ODD = DEPTH // 2
EPS = 1e-6

GM_HEADS = 4
GM_WIDTH = D_MODEL
GM_HEAD_DIM = GM_WIDTH // GM_HEADS
GM_CHUNK = 128

SSM_WIDTH = D_MODEL
SSM_HEAD_DIM = 64
SSM_HEADS = SSM_WIDTH // SSM_HEAD_DIM
SSM_GROUPS = 4
SSM_HEADS_PER_GROUP = SSM_HEADS // SSM_GROUPS
SSM_STATE = 128
SSM_CONV = 4
SSM_CHUNK = 128
SSM_CONV_DIM = SSM_WIDTH + 2 * SSM_GROUPS * SSM_STATE
DT_MIN = 0.001
DT_MAX = 0.1
DT_FLOOR = 1e-4

IN_PROJ_DIM = 2 * GM_WIDTH + SSM_WIDTH + SSM_CONV_DIM + SSM_HEADS
SPLIT_POINTS = (GM_WIDTH, 2 * GM_WIDTH, 2 * GM_WIDTH + SSM_WIDTH, 2 * GM_WIDTH + SSM_WIDTH + SSM_CONV_DIM)
MIX_WIDTH = GM_WIDTH + SSM_WIDTH

POOL_WINDOWS = (2, 4, 8, 16)
POOL_GROUPS = len(POOL_WINDOWS)
POOL_GROUP_DIM = D_MODEL // POOL_GROUPS

D_FF = ((8 * D_MODEL // 3 + 255) // 256) * 256

kernel_name = "hybrid_gmlp_ssd_pool_decoder"


def rms_norm(x, g):
    xf = x.astype(jnp.float32)
    y = xf * lax.rsqrt(jnp.mean(xf * xf, axis=-1, keepdims=True) + EPS)
    return (y * g.astype(jnp.float32)).astype(x.dtype)


def layer_norm(x, g, b):
    xf = x.astype(jnp.float32)
    mu = jnp.mean(xf, axis=-1, keepdims=True)
    xc = xf - mu
    y = xc * lax.rsqrt(jnp.mean(xc * xc, axis=-1, keepdims=True) + EPS)
    return (y * g.astype(jnp.float32) + b.astype(jnp.float32)).astype(x.dtype)


def gmlp_spatial_gating(u, v, ln_g, ln_b, w_s, b_s):
    bsz, seqlen, _ = v.shape
    n_chunks = seqlen // GM_CHUNK
    v = layer_norm(v, ln_g, ln_b).reshape(bsz, n_chunks, GM_CHUNK, GM_HEADS, GM_HEAD_DIM)
    causal = jnp.tril(jnp.ones((GM_CHUNK, GM_CHUNK), dtype=bool))
    w = jnp.where(causal[None], w_s, 0).astype(v.dtype)
    mixed = jnp.einsum("hts,bcshd->bcthd", w, v) + b_s.T.astype(v.dtype)[None, None, :, :, None]
    return u * mixed.reshape(bsz, seqlen, GM_WIDTH)


def causal_depthwise_conv(x, w, b):
    channels = x.shape[-1]
    y = lax.conv_general_dilated(
        x, w[:, None, :].astype(x.dtype), window_strides=(1,), padding=[(SSM_CONV - 1, 0)],
        dimension_numbers=("NWC", "WIO", "NWC"), feature_group_count=channels)
    return y + b.astype(x.dtype)


def segsum_exp(a_cum):
    n = a_cum.shape[-1]
    diff = a_cum[..., :, None] - a_cum[..., None, :]
    mask = jnp.tril(jnp.ones((n, n), dtype=bool))
    return jnp.exp(jnp.where(mask, diff, -jnp.inf))


def ssd_chunked(x, dt, a, b_mat, c_mat):
    bsz, seqlen = x.shape[:2]
    nc = seqlen // SSM_CHUNK

    def chunk(t):
        return t.reshape((bsz, nc, SSM_CHUNK) + t.shape[2:])

    xdt = chunk(x * dt[..., None])
    a_cum = jnp.cumsum(jnp.moveaxis(chunk(dt * a), 2, -1), axis=-1)
    b_c, c_c = chunk(b_mat), chunk(c_mat)
    decay = segsum_exp(a_cum)
    cb = jnp.einsum("bclgn,bcsgn->bcgls", c_c, b_c)
    y_diag = jnp.einsum("bcgls,bcgrls,bcsgrp->bclgrp", cb, decay, xdt)
    decay_to_end = jnp.exp(a_cum[..., -1:] - a_cum)
    chunk_states = jnp.einsum("bclgn,bcgrl,bclgrp->bcgrpn", b_c, decay_to_end, xdt)
    chunk_decay = jnp.exp(a_cum[..., -1])

    def step(state, inp):
        dec, new = inp
        return state * dec[..., None, None] + new, state

    init = jnp.zeros_like(chunk_states[:, 0])
    _, prev_states = lax.scan(step, init, (jnp.moveaxis(chunk_decay, 1, 0), jnp.moveaxis(chunk_states, 1, 0)))
    prev_states = jnp.moveaxis(prev_states, 0, 1)
    y_off = jnp.einsum("bclgn,bcgrpn,bcgrl->bclgrp", c_c, prev_states, jnp.exp(a_cum))
    return (y_diag + y_off).reshape((bsz, seqlen) + x.shape[2:])


def hybrid_gmlp_ssd_mixer(h, w_in, gm_ln_g, gm_ln_b, gm_ws, gm_bs, conv_w, conv_b,
                          dt_bias, a_log, d_skip, ssm_norm_g, w_out):
    f32 = jnp.float32
    bsz, seqlen, _ = h.shape
    proj = h @ w_in
    u, v, z, xbc, dt_raw = jnp.split(proj, SPLIT_POINTS, axis=-1)
    y_a = gmlp_spatial_gating(jax.nn.gelu(u), jax.nn.gelu(v), gm_ln_g, gm_ln_b, gm_ws, gm_bs)
    xbc = jax.nn.silu(causal_depthwise_conv(xbc, conv_w, conv_b))
    xs, b_mat, c_mat = jnp.split(xbc, (SSM_WIDTH, SSM_WIDTH + SSM_GROUPS * SSM_STATE), axis=-1)
    dt = jax.nn.softplus(dt_raw.astype(f32) + dt_bias.astype(f32))
    a = -jnp.exp(a_log.astype(f32))
    xs_h = xs.astype(f32).reshape(bsz, seqlen, SSM_GROUPS, SSM_HEADS_PER_GROUP, SSM_HEAD_DIM)
    y = ssd_chunked(
        xs_h,
        dt.reshape(bsz, seqlen, SSM_GROUPS, SSM_HEADS_PER_GROUP),
        a.reshape(SSM_GROUPS, SSM_HEADS_PER_GROUP),
        b_mat.astype(f32).reshape(bsz, seqlen, SSM_GROUPS, SSM_STATE),
        c_mat.astype(f32).reshape(bsz, seqlen, SSM_GROUPS, SSM_STATE))
    y = y + d_skip.astype(f32).reshape(SSM_GROUPS, SSM_HEADS_PER_GROUP)[:, :, None] * xs_h
    gated = (y.reshape(bsz, seqlen, SSM_WIDTH) * jax.nn.silu(z.astype(f32)))
    gated = gated.reshape(bsz, seqlen, SSM_GROUPS, SSM_WIDTH // SSM_GROUPS)
    gated = gated * lax.rsqrt(jnp.mean(gated * gated, axis=-1, keepdims=True) + EPS)
    y_b = (gated.reshape(bsz, seqlen, SSM_WIDTH) * ssm_norm_g.astype(f32)).astype(h.dtype)
    return jnp.concatenate([y_a, y_b], axis=-1) @ w_out


def multiscale_pool_mixer(h, pool_w, pool_b, pool_scale):
    f32 = jnp.float32
    bsz, seqlen, _ = h.shape
    hf = h.astype(f32).reshape(bsz, seqlen, POOL_GROUPS, POOL_GROUP_DIM)
    cs = jnp.cumsum(hf, axis=1)
    cs = jnp.concatenate([jnp.zeros_like(cs[:, :1]), cs], axis=1)
    pos = jnp.arange(1, seqlen + 1, dtype=f32)
    pooled = []
    for g, win in enumerate(POOL_WINDOWS):
        cs_g = cs[:, :, g]
        upper = cs_g[:, 1:]
        lower = jnp.pad(cs_g, ((0, 0), (win - 1, 0), (0, 0)))[:, :seqlen]
        count = jnp.minimum(pos, float(win))[None, :, None]
        pooled.append((upper - lower) / count)
    pooled = jnp.stack(pooled, axis=2)
    out = jnp.einsum("blgc,gcd->blgd", pooled - hf, pool_w.astype(f32)) + pool_b.astype(f32)
    return (out.reshape(bsz, seqlen, D_MODEL) * pool_scale.astype(f32)).astype(h.dtype)


def swiglu(h, w_gate, w_up, w_down):
    return (jax.nn.silu(h @ w_gate) * (h @ w_up)) @ w_down


def setup_inputs(seed: int = 0) -> dict:
    key = jax.random.key(seed)
    ks = jax.random.split(key, 20)
    f32 = jnp.float32

    def nrm(k, shape, scale):
        return jax.random.normal(k, shape, f32) * scale

    x = nrm(ks[0], (BATCH, SEQ, D_MODEL), 1.0)
    norm_g = 1.0 + nrm(ks[1], (DEPTH, 4, D_MODEL), 0.02)
    w_in = nrm(ks[2], (N_EVEN, D_MODEL, IN_PROJ_DIM), D_MODEL ** -0.5)
    gm_ln_g = 1.0 + nrm(ks[3], (N_EVEN, GM_WIDTH), 0.02)
    gm_ln_b = nrm(ks[4], (N_EVEN, GM_WIDTH), 0.02)
    gm_ws = nrm(ks[5], (N_EVEN, GM_HEADS, GM_CHUNK, GM_CHUNK), GM_CHUNK ** -0.5)
    gm_bs = 1.0 + nrm(ks[6], (N_EVEN, GM_HEADS, GM_CHUNK), 0.02)
    conv_w = nrm(ks[7], (N_EVEN, SSM_CONV, SSM_CONV_DIM), SSM_CONV ** -0.5)
    conv_b = nrm(ks[8], (N_EVEN, SSM_CONV_DIM), 0.02)
    dt0 = jnp.exp(jax.random.uniform(ks[9], (N_EVEN, SSM_HEADS), f32, math.log(DT_MIN), math.log(DT_MAX)))
    dt0 = jnp.maximum(dt0, DT_FLOOR)
    dt_bias = dt0 + jnp.log(-jnp.expm1(-dt0))
    a_log = jnp.log(jax.random.uniform(ks[10], (N_EVEN, SSM_HEADS), f32, 1.0, 16.0))
    d_skip = 1.0 + nrm(ks[11], (N_EVEN, SSM_HEADS), 0.02)
    ssm_norm_g = 1.0 + nrm(ks[12], (N_EVEN, SSM_WIDTH), 0.02)
    w_out = nrm(ks[13], (N_EVEN, MIX_WIDTH, D_MODEL), MIX_WIDTH ** -0.5)
    pool_w = nrm(ks[14], (N_ODD, POOL_GROUPS, POOL_GROUP_DIM, POOL_GROUP_DIM), POOL_GROUP_DIM ** -0.5)
    pool_b = nrm(ks[15], (N_ODD, POOL_GROUPS, POOL_GROUP_DIM), 0.02)
    pool_scale = 1.0 + nrm(ks[16], (N_ODD, D_MODEL), 0.1)
    ffn_w_gate = nrm(ks[17], (DEPTH, D_MODEL, D_FF), D_MODEL ** -0.5)
    ffn_w_up = nrm(ks[18], (DEPTH, D_MODEL, D_FF), D_MODEL ** -0.5)
    ffn_w_down = nrm(ks[19], (DEPTH, D_FF, D_MODEL), D_FF ** -0.5)
    return {"x": x, "norm_g": norm_g, "w_in": w_in, "gm_ln_g": gm_ln_g, "gm_ln_b": gm_ln_b,
            "gm_ws": gm_ws, "gm_bs": gm_bs, "conv_w": conv_w, "conv_b": conv_b,
            "dt_bias": dt_bias, "a_log": a_log, "d_skip": d_skip, "ssm_norm_g": ssm_norm_g,
            "w_out": w_out, "pool_w": pool_w, "pool_b": pool_b, "pool_scale": pool_scale,
            "ffn_w_gate": ffn_w_gate, "ffn_w_up": ffn_w_up, "ffn_w_down": ffn_w_down}


def reference(x, norm_g, w_in, gm_ln_g, gm_ln_b, gm_ws, gm_bs, conv_w, conv_b, dt_bias, a_log,
              d_skip, ssm_norm_g, w_out, pool_w, pool_b, pool_scale, ffn_w_gate, ffn_w_up, ffn_w_down):
    h = x
    for layer in range(DEPTH):
        i = layer // 2
        y = rms_norm(h, norm_g[layer, 0])
        if layer % 2 == 0:
            y = hybrid_gmlp_ssd_mixer(y, w_in[i], gm_ln_g[i], gm_ln_b[i], gm_ws[i], gm_bs[i],
                                      conv_w[i], conv_b[i], dt_bias[i], a_log[i], d_skip[i],
                                      ssm_norm_g[i], w_out[i])
        else:
            y = multiscale_pool_mixer(y, pool_w[i], pool_b[i], pool_scale[i])
        h = h + rms_norm(y, norm_g[layer, 1])
        y = swiglu(rms_norm(h, norm_g[layer, 2]), ffn_w_gate[layer], ffn_w_up[layer], ffn_w_down[layer])
        h = h + rms_norm(y, norm_g[layer, 3])
    return h
```

```python
import functools
import math

import jax
import jax.numpy as jnp
from jax import lax
from jax.experimental import pallas as pl
from jax.experimental.pallas import tpu as pltpu

F32 = jnp.float32
BF16 = jnp.bfloat16

EPS = 1e-6
D_MODEL = 1024
CHUNK = 128
GM_HEADS = 4
GM_HEAD_DIM = D_MODEL // GM_HEADS
SSM_HEADS = 16
SSM_HEAD_DIM = 64
SSM_GROUPS = 4
SSM_HEADS_PER_GROUP = SSM_HEADS // SSM_GROUPS
SSM_GROUP_DIM = SSM_HEADS_PER_GROUP * SSM_HEAD_DIM
SSM_STATE = 128
SSM_CONV = 4
POOL_WINDOWS = (2, 4, 8, 16)
POOL_GROUP_DIM = D_MODEL // len(POOL_WINDOWS)
D_FF = 2816

V7X_LANES = 128
V7X_SUBLANES = 8
V7X_MXU_DIM = 256
V7X_VMEM_BYTES = 64 * 1024 * 1024

COL_V = D_MODEL
COL_Z = 2 * D_MODEL
COL_XBC = 3 * D_MODEL
COL_DT = 5 * D_MODEL
XBC_DIM = 2 * D_MODEL
FF_BLOCK = V7X_MXU_DIM
CONV_PAD = V7X_SUBLANES
POOL_PAD = 16


def _tile_plan():
    return dict(mixer_tile=256, ffn_tile=512, vmem_limit=56 * 1024 * 1024)


def _dot(a, b):
    return jnp.dot(a, b, preferred_element_type=F32)


def _rms(x, g):
    return x * lax.rsqrt(jnp.mean(x * x, axis=-1, keepdims=True) + EPS) * g


def _sigmoid(x):
    return 0.5 * (1.0 + jnp.tanh(0.5 * x))


def _silu(x):
    return x * _sigmoid(x)


def _gelu_tanh(x):
    c = math.sqrt(2.0 / math.pi)
    return x * (0.5 * (1.0 + jnp.tanh(c * (x + 0.044715 * (x * x * x)))))


def _softplus(x):
    return jnp.maximum(x, 0.0) + jnp.log1p(jnp.exp(-jnp.abs(x)))


def _split3_bf16(x):
    hi = x.astype(BF16)
    r1 = x - hi.astype(F32)
    mid = r1.astype(BF16)
    lo = (r1 - mid.astype(F32)).astype(BF16)
    return hi, mid, lo


def _ssd_chunk(xs, bm, cm, dt, a_row, dskip, state_ref):
    row = lax.broadcasted_iota(jnp.int32, (CHUNK, CHUNK), 0)
    col = lax.broadcasted_iota(jnp.int32, (CHUNK, CHUNK), 1)
    causal = row >= col
    tril = jnp.where(causal, 1.0, 0.0).astype(BF16)
    hi, mid, lo = _split3_bf16(dt * a_row)
    a_cum = _dot(tril, hi) + _dot(tril, mid) + _dot(tril, lo)
    a_cum_t = a_cum.T
    dt_t = dt.T
    w_t = dt_t * jnp.exp(a_cum_t[:, CHUNK - 1:CHUNK] - a_cum_t)
    head_of_lane = lax.broadcasted_iota(jnp.int32, (1, SSM_GROUP_DIM), 1) // SSM_HEAD_DIM

    ys = []
    for g in range(SSM_GROUPS):
        b_g = bm[:, g * SSM_STATE:(g + 1) * SSM_STATE]
        c_g = cm[:, g * SSM_STATE:(g + 1) * SSM_STATE]
        xs_g = xs[:, g * SSM_GROUP_DIM:(g + 1) * SSM_GROUP_DIM]
        cb = lax.dot_general(c_g.astype(BF16), b_g.astype(BF16), (((1,), (1,)), ((), ())),
                             preferred_element_type=F32)
        b_t = b_g.T
        state = state_ref[g]
        m_parts, ec_parts, bw_parts, xs_parts, st_parts = [], [], [], [], []
        cd_row = jnp.zeros((1, SSM_GROUP_DIM), F32)
        for r in range(SSM_HEADS_PER_GROUP):
            h = g * SSM_HEADS_PER_GROUP + r
            bc = jnp.broadcast_to(a_cum[:, h:h + 1], (CHUNK, CHUNK))
            decay = jnp.exp(jnp.where(causal, bc - a_cum_t[h:h + 1, :], -jnp.inf))
            m_parts.append((cb * decay * dt_t[h:h + 1, :]).astype(BF16))
            ec_parts.append((c_g * jnp.exp(bc)).astype(BF16))
            bw_parts.append((b_t * w_t[h:h + 1, :]).astype(BF16))
            in_head = head_of_lane == r
            xs_parts.append(jnp.where(in_head, xs_g, 0.0).astype(BF16))
            st_parts.append(jnp.where(in_head, state, 0.0).astype(BF16))
            cd_row = cd_row + jnp.where(in_head, a_cum[CHUNK - 1:CHUNK, h:h + 1], 0.0)
        xs_stack = jnp.concatenate(xs_parts, axis=0)
        lhs_y = jnp.concatenate(m_parts + ec_parts, axis=1)
        rhs_y = jnp.concatenate([xs_stack] + st_parts, axis=0)
        y_g = _dot(lhs_y, rhs_y)
        new = _dot(jnp.concatenate(bw_parts, axis=1), xs_stack)
        state_ref[g] = state * jnp.exp(cd_row) + new
        ys.append(y_g)
    return jnp.concatenate(ys, axis=1) + dskip * xs


def _mixer0_kernel(x_ref, gpre_ref, gpost_ref, win_ref, wdt_ref, lng_ref, lnb_ref, ws_ref,
                   bmap_ref, convw_ref, convb_ref, dtb_ref, alog_ref, dskip_ref, sng_ref,
                   wout_ref, o_ref, xbc_buf, state_ref, mix_buf, *, tile):
    @pl.when(pl.program_id(0) == 0)
    def _():
        xbc_buf[0:CONV_PAD, :] = jnp.zeros((CONV_PAD, XBC_DIM), F32)
        state_ref[...] = jnp.zeros(state_ref.shape, F32)

    x = x_ref[...]
    xn = _rms(x, gpre_ref[...]).astype(BF16)

    gu = _gelu_tanh(_dot(xn, win_ref[:, 0:COL_V]))
    gv = _gelu_tanh(_dot(xn, win_ref[:, COL_V:COL_Z]))
    mu = jnp.mean(gv, axis=-1, keepdims=True)
    gc = gv - mu
    vln = (gc * lax.rsqrt(jnp.mean(gc * gc, axis=-1, keepdims=True) + EPS) * lng_ref[...]
           + lnb_ref[...]).astype(BF16)
    row = lax.broadcasted_iota(jnp.int32, (CHUNK, CHUNK), 0)
    col = lax.broadcasted_iota(jnp.int32, (CHUNK, CHUNK), 1)
    w_causal = [jnp.where(row >= col, ws_ref[h], 0.0).astype(BF16) for h in range(GM_HEADS)]
    for c in range(tile // CHUNK):
        rows = slice(c * CHUNK, (c + 1) * CHUNK)
        mixed = jnp.concatenate(
            [_dot(w_causal[h], vln[rows, h * GM_HEAD_DIM:(h + 1) * GM_HEAD_DIM])
             for h in range(GM_HEADS)], axis=1)
        mix_buf[rows, 0:D_MODEL] = (gu[rows] * (mixed + bmap_ref[...])).astype(BF16)

    z = _dot(xn, win_ref[:, COL_Z:COL_XBC])
    xbc_buf[CONV_PAD:CONV_PAD + tile, :] = _dot(xn, win_ref[:, COL_XBC:COL_DT])
    conv = convb_ref[...]
    for k in range(SSM_CONV):
        start = CONV_PAD - (SSM_CONV - 1) + k
        conv = conv + convw_ref[k:k + 1, :] * xbc_buf[start:start + tile, :]
    xbc_buf[0:CONV_PAD, :] = xbc_buf[tile:tile + CONV_PAD, :]
    xbc = _silu(conv)
    dt = _softplus(_dot(xn, wdt_ref[...]) + dtb_ref[...])
    a_row = -jnp.exp(alog_ref[...])
    for c in range(tile // CHUNK):
        rows = slice(c * CHUNK, (c + 1) * CHUNK)
        y = _ssd_chunk(xbc[rows, 0:D_MODEL], xbc[rows, D_MODEL:D_MODEL + SSM_GROUPS * SSM_STATE],
                       xbc[rows, D_MODEL + SSM_GROUPS * SSM_STATE:XBC_DIM], dt[rows], a_row,
                       dskip_ref[...], state_ref)
        gated = y * _silu(z[rows])
        parts = []
        for g in range(SSM_GROUPS):
            gg = gated[:, g * SSM_GROUP_DIM:(g + 1) * SSM_GROUP_DIM]
            parts.append(gg * lax.rsqrt(jnp.mean(gg * gg, axis=-1, keepdims=True) + EPS))
        mix_buf[rows, D_MODEL:2 * D_MODEL] = (
            jnp.concatenate(parts, axis=1) * sng_ref[...]).astype(BF16)

    out = _dot(mix_buf[...], wout_ref[...])
    o_ref[...] = x + _rms(out, gpost_ref[...])


def _const_spec(shape):
    nd = len(shape)
    return pl.BlockSpec(shape, lambda i: (0,) * nd, pipeline_mode=pl.Buffered(1))


def _mixer0(x, gpre, gpost, w_in, w_dt, ln_g, ln_b, ws, bmap, conv_w, conv_b, dt_bias, a_log,
            dskip, sng, w_out):
    plan = _tile_plan()
    tile = plan["mixer_tile"]
    seq = x.shape[0]
    consts = (gpre, gpost, w_in, w_dt, ln_g, ln_b, ws, bmap, conv_w, conv_b, dt_bias, a_log,
              dskip, sng, w_out)
    tok_spec = pl.BlockSpec((tile, D_MODEL), lambda i: (i, 0))
    return pl.pallas_call(
        functools.partial(_mixer0_kernel, tile=tile),
        grid=(seq // tile,),
        in_specs=[tok_spec] + [_const_spec(c.shape) for c in consts],
        out_specs=tok_spec,
        out_shape=jax.ShapeDtypeStruct(x.shape, F32),
        scratch_shapes=[
            pltpu.VMEM((CONV_PAD + tile, XBC_DIM), F32),
            pltpu.VMEM((SSM_GROUPS, SSM_STATE, SSM_GROUP_DIM), F32),
            pltpu.VMEM((tile, 2 * D_MODEL), BF16),
        ],
        compiler_params=pltpu.CompilerParams(
            dimension_semantics=("arbitrary",), vmem_limit_bytes=plan["vmem_limit"]),
        name="mixer0",
    )(x, *consts)


def _ffn_stage(h, gpre_ref, gpost_ref, wg_ref, wu_ref, wd_ref, act_buf):
    hn = _rms(h, gpre_ref[...]).astype(BF16)
    for j in range(D_FF // FF_BLOCK):
        cols = slice(j * FF_BLOCK, (j + 1) * FF_BLOCK)
        act_buf[:, cols] = (_silu(_dot(hn, wg_ref[:, cols])) * _dot(hn, wu_ref[:, cols])).astype(BF16)
    y = _dot(act_buf[...], wd_ref[...])
    return h + _rms(y, gpost_ref[...])


def _ffn_kernel(h_ref, gpre_ref, gpost_ref, wg_ref, wu_ref, wd_ref, o_ref, act_buf):
    o_ref[...] = _ffn_stage(h_ref[...], gpre_ref, gpost_ref, wg_ref, wu_ref, wd_ref, act_buf)


def _pool_ffn_kernel(h_ref, pgpre_ref, pgpost_ref, pw_ref, pb_ref, pscale_ref,
                     gpre_ref, gpost_ref, wg_ref, wu_ref, wd_ref, o_ref, y_buf, act_buf, *, tile):
    @pl.when(pl.program_id(0) == 0)
    def _():
        y_buf[0:POOL_PAD, :] = jnp.zeros((POOL_PAD, D_MODEL), F32)

    h = h_ref[...]
    y = _rms(h, pgpre_ref[...])
    y_buf[POOL_PAD:POOL_PAD + tile, :] = y
    pos = pl.program_id(0) * tile + lax.broadcasted_iota(jnp.int32, (tile, 1), 0) + 1
    outs = []
    for g, win in enumerate(POOL_WINDOWS):
        cols = slice(g * POOL_GROUP_DIM, (g + 1) * POOL_GROUP_DIM)
        wsum = y[:, cols]
        for j in range(1, win):
            wsum = wsum + y_buf[POOL_PAD - j:POOL_PAD - j + tile, cols]
        count = jnp.minimum(pos, win).astype(F32)
        d = wsum / count - y[:, cols]
        outs.append(_dot(d.astype(BF16), pw_ref[g]) + pb_ref[g:g + 1, :])
    y_buf[0:POOL_PAD, :] = y_buf[tile:tile + POOL_PAD, :]
    mixed = jnp.concatenate(outs, axis=1) * pscale_ref[...]
    h = h + _rms(mixed, pgpost_ref[...])
    o_ref[...] = _ffn_stage(h, gpre_ref, gpost_ref, wg_ref, wu_ref, wd_ref, act_buf)


def _ffn(h, gpre, gpost, wg, wu, wd, pool=None):
    plan = _tile_plan()
    tile = plan["ffn_tile"]
    seq = h.shape[0]
    tok_spec = pl.BlockSpec((tile, D_MODEL), lambda i: (i, 0))
    consts = (gpre, gpost, wg, wu, wd)
    scratch = [pltpu.VMEM((tile, D_FF), BF16)]
    if pool is None:
        body, name = _ffn_kernel, "ffn"
    else:
        body, name = functools.partial(_pool_ffn_kernel, tile=tile), "pool_ffn"
        consts = tuple(pool) + consts
        scratch = [pltpu.VMEM((POOL_PAD + tile, D_MODEL), F32)] + scratch
    return pl.pallas_call(
        body,
        grid=(seq // tile,),
        in_specs=[tok_spec] + [_const_spec(c.shape) for c in consts],
        out_specs=tok_spec,
        out_shape=jax.ShapeDtypeStruct(h.shape, F32),
        scratch_shapes=scratch,
        compiler_params=pltpu.CompilerParams(
            dimension_semantics=("arbitrary",), vmem_limit_bytes=plan["vmem_limit"]),
        name=name,
    )(h, *consts)


def _row(v):
    return v.reshape(1, -1).astype(F32)


def _pad_lanes(v):
    return jnp.pad(v, [(0, 0)] * (v.ndim - 1) + [(0, V7X_LANES - v.shape[-1])])


def kernel(x, norm_g, w_in, gm_ln_g, gm_ln_b, gm_ws, gm_bs, conv_w, conv_b, dt_bias, a_log, d_skip,
           ssm_norm_g, w_out, pool_w, pool_b, pool_scale, ffn_w_gate, ffn_w_up, ffn_w_down):
    bsz, seq, _ = x.shape
    assert bsz == 1 and norm_g.shape[0] == 2
    h = x.reshape(seq, D_MODEL)

    w_in0 = w_in[0]
    bmap = jnp.repeat(gm_bs[0].T, GM_HEAD_DIM, axis=1)
    h = _mixer0(
        h, _row(norm_g[0, 0]), _row(norm_g[0, 1]),
        w_in0[:, :COL_DT].astype(BF16), _pad_lanes(w_in0[:, COL_DT:]).astype(BF16),
        _row(gm_ln_g[0]), _row(gm_ln_b[0]), gm_ws[0], bmap, conv_w[0], _row(conv_b[0]),
        _pad_lanes(_row(dt_bias[0])), _pad_lanes(_row(a_log[0])),
        _row(jnp.repeat(d_skip[0], SSM_HEAD_DIM)), _row(ssm_norm_g[0]), w_out[0].astype(BF16))
    h = _ffn(h, _row(norm_g[0, 2]), _row(norm_g[0, 3]), ffn_w_gate[0].astype(BF16),
             ffn_w_up[0].astype(BF16), ffn_w_down[0].astype(BF16))

    pool = (_row(norm_g[1, 0]), _row(norm_g[1, 1]), pool_w[0].astype(BF16), pool_b[0],
            _row(pool_scale[0]))
    h = _ffn(h, _row(norm_g[1, 2]), _row(norm_g[1, 3]), ffn_w_gate[1].astype(BF16),
             ffn_w_up[1].astype(BF16), ffn_w_down[1].astype(BF16), pool=pool)
    return h.reshape(bsz, seq, D_MODEL)
```

```python
import functools
import math

import jax
import jax.numpy as jnp
from jax import lax
from jax.experimental import pallas as pl
from jax.experimental.pallas import tpu as pltpu

F32 = jnp.float32
BF16 = jnp.bfloat16

EPS = 1e-6
D_MODEL = 1024
CHUNK = 128
GM_HEADS = 4
GM_HEAD_DIM = D_MODEL // GM_HEADS
SSM_HEADS = 16
SSM_HEAD_DIM = 64
SSM_GROUPS = 4
SSM_HEADS_PER_GROUP = SSM_HEADS // SSM_GROUPS
SSM_GROUP_DIM = SSM_HEADS_PER_GROUP * SSM_HEAD_DIM
SSM_STATE = 128
SSM_CONV = 4
POOL_WINDOWS = (2, 4, 8, 16)
POOL_GROUP_DIM = D_MODEL // len(POOL_WINDOWS)
D_FF = 2816

V7X_LANES = 128
V7X_SUBLANES = 8
V7X_MXU_DIM = 256
V7X_VMEM_BYTES = 64 * 1024 * 1024

COL_V = D_MODEL
COL_Z = 2 * D_MODEL
COL_XBC = 3 * D_MODEL
COL_DT = 5 * D_MODEL
XBC_DIM = 2 * D_MODEL
FF_BLOCK = V7X_MXU_DIM
CONV_PAD = V7X_SUBLANES
POOL_PAD = CHUNK


def _tile_plan():
    return dict(mixer_tile=256, ffn_tile=512, vmem_limit=56 * 1024 * 1024)


def _dot(a, b):
    return jnp.dot(a, b, preferred_element_type=F32)


def _rms(x, g):
    return x * lax.rsqrt(jnp.mean(x * x, axis=-1, keepdims=True) + EPS) * g


def _sigmoid(x):
    return 0.5 * (1.0 + jnp.tanh(0.5 * x))


def _silu(x):
    return x * _sigmoid(x)


def _gelu_tanh(x):
    c = math.sqrt(2.0 / math.pi)
    return x * (0.5 * (1.0 + jnp.tanh(c * (x + 0.044715 * (x * x * x)))))


def _softplus(x):
    return jnp.maximum(x, 0.0) + jnp.log1p(jnp.exp(-jnp.abs(x)))


def _split3_bf16(x):
    hi = x.astype(BF16)
    r1 = x - hi.astype(F32)
    mid = r1.astype(BF16)
    lo = (r1 - mid.astype(F32)).astype(BF16)
    return hi, mid, lo


def _ssd_chunk(xs, bm, cm, dt, a_row, dskip, state_ref):
    row = lax.broadcasted_iota(jnp.int32, (CHUNK, CHUNK), 0)
    col = lax.broadcasted_iota(jnp.int32, (CHUNK, CHUNK), 1)
    causal = row >= col
    tril = jnp.where(causal, 1.0, 0.0).astype(BF16)
    hi, mid, lo = _split3_bf16(dt * a_row)
    a_cum = _dot(tril, hi) + _dot(tril, mid) + _dot(tril, lo)
    a_cum_t = a_cum.T
    dt_t = dt.T
    w_t = dt_t * jnp.exp(a_cum_t[:, CHUNK - 1:CHUNK] - a_cum_t)
    a_sub_t = a_cum_t - jnp.log(dt_t)
    head_of_lane = lax.broadcasted_iota(jnp.int32, (1, SSM_GROUP_DIM), 1) // SSM_HEAD_DIM

    ys = []
    for g in range(SSM_GROUPS):
        b_g = bm[:, g * SSM_STATE:(g + 1) * SSM_STATE]
        c_g = cm[:, g * SSM_STATE:(g + 1) * SSM_STATE]
        xs_g = xs[:, g * SSM_GROUP_DIM:(g + 1) * SSM_GROUP_DIM]
        cb = lax.dot_general(c_g.astype(BF16), b_g.astype(BF16), (((1,), (1,)), ((), ())),
                             preferred_element_type=F32)
        b_t = b_g.T
        state = state_ref[g]
        xs_bf, state_bf = xs_g.astype(BF16), state.astype(BF16)
        m_parts, ec_parts, bw_parts, xs_parts, st_parts = [], [], [], [], []
        cd_row = jnp.zeros((1, SSM_GROUP_DIM), F32)
        for r in range(SSM_HEADS_PER_GROUP):
            h = g * SSM_HEADS_PER_GROUP + r
            bc = jnp.broadcast_to(a_cum[:, h:h + 1], (CHUNK, CHUNK))
            decay_dt = jnp.exp(jnp.where(causal, bc - a_sub_t[h:h + 1, :], -jnp.inf))
            m_parts.append((cb * decay_dt).astype(BF16))
            ec_parts.append((c_g * jnp.exp(bc)).astype(BF16))
            bw_parts.append((b_t * w_t[h:h + 1, :]).astype(BF16))
            in_head = head_of_lane == r
            xs_parts.append(jnp.where(in_head, xs_bf, 0.0).astype(BF16))
            st_parts.append(jnp.where(in_head, state_bf, 0.0).astype(BF16))
            cd_row = cd_row + jnp.where(in_head, a_cum[CHUNK - 1:CHUNK, h:h + 1], 0.0)
        xs_stack = jnp.concatenate(xs_parts, axis=0)
        lhs_y = jnp.concatenate(m_parts + ec_parts, axis=1)
        rhs_y = jnp.concatenate([xs_stack] + st_parts, axis=0)
        y_g = _dot(lhs_y, rhs_y)
        new = _dot(jnp.concatenate(bw_parts, axis=1), xs_stack)
        state_ref[g] = state * jnp.exp(cd_row) + new
        ys.append(y_g)
    return jnp.concatenate(ys, axis=1) + dskip * xs


def _mixer0_kernel(x_ref, gpre_ref, gpost_ref, win_ref, wdt_ref, lng_ref, lnb_ref, ws_ref,
                   bmap_ref, convw_ref, convb_ref, dtb_ref, alog_ref, dskip_ref, sng_ref,
                   wout_ref, o_ref, xbc_buf, state_ref, mix_buf, *, tile):
    @pl.when(pl.program_id(0) == 0)
    def _():
        xbc_buf[0:CONV_PAD, :] = jnp.zeros((CONV_PAD, XBC_DIM), F32)
        state_ref[...] = jnp.zeros(state_ref.shape, F32)

    x = x_ref[...]
    xn = _rms(x, gpre_ref[...]).astype(BF16)

    gu = _gelu_tanh(_dot(xn, win_ref[:, 0:COL_V]))
    gv = _gelu_tanh(_dot(xn, win_ref[:, COL_V:COL_Z]))
    mu = jnp.mean(gv, axis=-1, keepdims=True)
    gc = gv - mu
    vln = (gc * lax.rsqrt(jnp.mean(gc * gc, axis=-1, keepdims=True) + EPS) * lng_ref[...]
           + lnb_ref[...]).astype(BF16)
    row = lax.broadcasted_iota(jnp.int32, (CHUNK, CHUNK), 0)
    col = lax.broadcasted_iota(jnp.int32, (CHUNK, CHUNK), 1)
    w_causal = [jnp.where(row >= col, ws_ref[h], 0.0).astype(BF16) for h in range(GM_HEADS)]
    for c in range(tile // CHUNK):
        rows = slice(c * CHUNK, (c + 1) * CHUNK)
        mixed = jnp.concatenate(
            [_dot(w_causal[h], vln[rows, h * GM_HEAD_DIM:(h + 1) * GM_HEAD_DIM])
             for h in range(GM_HEADS)], axis=1)
        mix_buf[rows, 0:D_MODEL] = (gu[rows] * (mixed + bmap_ref[...])).astype(BF16)

    z = _dot(xn, win_ref[:, COL_Z:COL_XBC])
    xbc_buf[CONV_PAD:CONV_PAD + tile, :] = _dot(xn, win_ref[:, COL_XBC:COL_DT])
    ext = xbc_buf[...]
    conv = convb_ref[...] + convw_ref[SSM_CONV - 1:SSM_CONV, :] * ext[CONV_PAD:, :]
    for k in range(SSM_CONV - 1):
        shifted = pltpu.roll(ext, SSM_CONV - 1 - k, axis=0)
        conv = conv + convw_ref[k:k + 1, :] * shifted[CONV_PAD:, :]
    xbc_buf[0:CONV_PAD, :] = ext[tile:tile + CONV_PAD, :]
    xbc = _silu(conv)
    dt = _softplus(_dot(xn, wdt_ref[...]) + dtb_ref[...])
    a_row = -jnp.exp(alog_ref[...])
    for c in range(tile // CHUNK):
        rows = slice(c * CHUNK, (c + 1) * CHUNK)
        y = _ssd_chunk(xbc[rows, 0:D_MODEL], xbc[rows, D_MODEL:D_MODEL + SSM_GROUPS * SSM_STATE],
                       xbc[rows, D_MODEL + SSM_GROUPS * SSM_STATE:XBC_DIM], dt[rows], a_row,
                       dskip_ref[...], state_ref)
        gated = y * _silu(z[rows])
        parts = []
        for g in range(SSM_GROUPS):
            gg = gated[:, g * SSM_GROUP_DIM:(g + 1) * SSM_GROUP_DIM]
            parts.append(gg * lax.rsqrt(jnp.mean(gg * gg, axis=-1, keepdims=True) + EPS))
        mix_buf[rows, D_MODEL:2 * D_MODEL] = (
            jnp.concatenate(parts, axis=1) * sng_ref[...]).astype(BF16)

    out = _dot(mix_buf[...], wout_ref[...])
    o_ref[...] = x + _rms(out, gpost_ref[...])


def _const_spec(shape):
    nd = len(shape)
    return pl.BlockSpec(shape, lambda i: (0,) * nd, pipeline_mode=pl.Buffered(1))


def _mixer0(x, gpre, gpost, w_in, w_dt, ln_g, ln_b, ws, bmap, conv_w, conv_b, dt_bias, a_log,
            dskip, sng, w_out):
    plan = _tile_plan()
    tile = plan["mixer_tile"]
    seq = x.shape[0]
    consts = (gpre, gpost, w_in, w_dt, ln_g, ln_b, ws, bmap, conv_w, conv_b, dt_bias, a_log,
              dskip, sng, w_out)
    tok_spec = pl.BlockSpec((tile, D_MODEL), lambda i: (i, 0))
    return pl.pallas_call(
        functools.partial(_mixer0_kernel, tile=tile),
        grid=(seq // tile,),
        in_specs=[tok_spec] + [_const_spec(c.shape) for c in consts],
        out_specs=tok_spec,
        out_shape=jax.ShapeDtypeStruct(x.shape, F32),
        scratch_shapes=[
            pltpu.VMEM((CONV_PAD + tile, XBC_DIM), F32),
            pltpu.VMEM((SSM_GROUPS, SSM_STATE, SSM_GROUP_DIM), F32),
            pltpu.VMEM((tile, 2 * D_MODEL), BF16),
        ],
        compiler_params=pltpu.CompilerParams(
            dimension_semantics=("arbitrary",), vmem_limit_bytes=plan["vmem_limit"]),
        name="mixer0",
    )(x, *consts)


def _ffn_stage(h, gpre_ref, gpost_ref, wg_ref, wu_ref, wd_ref, act_buf):
    hn = _rms(h, gpre_ref[...]).astype(BF16)
    for j in range(D_FF // FF_BLOCK):
        cols = slice(j * FF_BLOCK, (j + 1) * FF_BLOCK)
        act_buf[:, cols] = (_silu(_dot(hn, wg_ref[:, cols])) * _dot(hn, wu_ref[:, cols])).astype(BF16)
    y = _dot(act_buf[...], wd_ref[...])
    return h + _rms(y, gpost_ref[...])


def _ffn_kernel(h_ref, gpre_ref, gpost_ref, wg_ref, wu_ref, wd_ref, o_ref, act_buf):
    o_ref[...] = _ffn_stage(h_ref[...], gpre_ref, gpost_ref, wg_ref, wu_ref, wd_ref, act_buf)


def _pool_ffn_kernel(h_ref, pgpre_ref, pgpost_ref, pw_ref, pb_ref, pscale_ref,
                     gpre_ref, gpost_ref, wg_ref, wu_ref, wd_ref, o_ref, yhi_buf, ymid_buf, d_buf,
                     act_buf, *, tile):
    @pl.when(pl.program_id(0) == 0)
    def _():
        yhi_buf[0:POOL_PAD, :] = jnp.zeros((POOL_PAD, D_MODEL), BF16)
        ymid_buf[0:POOL_PAD, :] = jnp.zeros((POOL_PAD, D_MODEL), BF16)

    h = h_ref[...]
    y = _rms(h, pgpre_ref[...])
    y_hi = y.astype(BF16)
    yhi_buf[POOL_PAD:POOL_PAD + tile, :] = y_hi
    ymid_buf[POOL_PAD:POOL_PAD + tile, :] = (y - y_hi.astype(F32)).astype(BF16)
    lag = (lax.broadcasted_iota(jnp.int32, (CHUNK, 2 * CHUNK), 0) + CHUNK
           - lax.broadcasted_iota(jnp.int32, (CHUNK, 2 * CHUNK), 1))
    bands = [jnp.where((lag >= 0) & (lag < win), 1.0, 0.0).astype(BF16) for win in POOL_WINDOWS]
    for b in range(tile // CHUNK):
        rows = slice(b * CHUNK, (b + 1) * CHUNK)
        window = slice(b * CHUNK, (b + 2) * CHUNK)
        pos = (pl.program_id(0) * tile + b * CHUNK + 1
               + lax.broadcasted_iota(jnp.int32, (CHUNK, 1), 0))
        for g, win in enumerate(POOL_WINDOWS):
            cols = slice(g * POOL_GROUP_DIM, (g + 1) * POOL_GROUP_DIM)
            wsum = _dot(bands[g], yhi_buf[window, cols]) + _dot(bands[g], ymid_buf[window, cols])
            count = jnp.minimum(pos, win).astype(F32)
            d_buf[rows, cols] = (wsum / count - y[rows, cols]).astype(BF16)
    yhi_buf[0:POOL_PAD, :] = yhi_buf[tile:tile + POOL_PAD, :]
    ymid_buf[0:POOL_PAD, :] = ymid_buf[tile:tile + POOL_PAD, :]
    outs = []
    for g in range(len(POOL_WINDOWS)):
        cols = slice(g * POOL_GROUP_DIM, (g + 1) * POOL_GROUP_DIM)
        outs.append(_dot(d_buf[:, cols], pw_ref[g]) + pb_ref[g:g + 1, :])
    mixed = jnp.concatenate(outs, axis=1) * pscale_ref[...]
    h = h + _rms(mixed, pgpost_ref[...])
    o_ref[...] = _ffn_stage(h, gpre_ref, gpost_ref, wg_ref, wu_ref, wd_ref, act_buf)


def _layer_spec(shape, layer):
    nd = len(shape) - 1
    return pl.BlockSpec((None,) + tuple(shape[1:]), lambda i: (layer,) + (0,) * nd,
                        pipeline_mode=pl.Buffered(1))


def _ffn(h, layer, gpre, gpost, wg, wu, wd, pool=None):
    plan = _tile_plan()
    tile = plan["ffn_tile"]
    seq = h.shape[0]
    tok_spec = pl.BlockSpec((tile, D_MODEL), lambda i: (i, 0))
    consts = (gpre, gpost)
    stacked = (wg, wu, wd)
    scratch = [pltpu.VMEM((tile, D_FF), BF16)]
    if pool is None:
        body, name = _ffn_kernel, "ffn"
    else:
        body, name = functools.partial(_pool_ffn_kernel, tile=tile), "pool_ffn"
        consts = tuple(pool) + consts
        scratch = [pltpu.VMEM((POOL_PAD + tile, D_MODEL), BF16),
                   pltpu.VMEM((POOL_PAD + tile, D_MODEL), BF16),
                   pltpu.VMEM((tile, D_MODEL), BF16)] + scratch
    return pl.pallas_call(
        body,
        grid=(seq // tile,),
        in_specs=([tok_spec] + [_const_spec(c.shape) for c in consts]
                  + [_layer_spec(w.shape, layer) for w in stacked]),
        out_specs=tok_spec,
        out_shape=jax.ShapeDtypeStruct(h.shape, F32),
        scratch_shapes=scratch,
        compiler_params=pltpu.CompilerParams(
            dimension_semantics=("arbitrary",), vmem_limit_bytes=plan["vmem_limit"]),
        name=name,
    )(h, *consts, *stacked)


def _row(v):
    return v.reshape(1, -1).astype(F32)


def _pad_lanes(v):
    return jnp.pad(v, [(0, 0)] * (v.ndim - 1) + [(0, V7X_LANES - v.shape[-1])])


def kernel(x, norm_g, w_in, gm_ln_g, gm_ln_b, gm_ws, gm_bs, conv_w, conv_b, dt_bias, a_log, d_skip,
           ssm_norm_g, w_out, pool_w, pool_b, pool_scale, ffn_w_gate, ffn_w_up, ffn_w_down):
    bsz, seq, _ = x.shape
    assert bsz == 1 and norm_g.shape[0] == 2
    h = x.reshape(seq, D_MODEL)
    wg, wu, wd = (w.astype(BF16) for w in (ffn_w_gate, ffn_w_up, ffn_w_down))

    w_in0 = w_in[0]
    bmap = jnp.repeat(gm_bs[0].T, GM_HEAD_DIM, axis=1)
    h = _mixer0(
        h, _row(norm_g[0, 0]), _row(norm_g[0, 1]),
        w_in0.astype(BF16), _pad_lanes(w_in0[:, COL_DT:]).astype(BF16),
        _row(gm_ln_g[0]), _row(gm_ln_b[0]), gm_ws[0], bmap, conv_w[0], _row(conv_b[0]),
        _pad_lanes(_row(dt_bias[0])), _pad_lanes(_row(a_log[0])),
        _row(jnp.repeat(d_skip[0], SSM_HEAD_DIM)), _row(ssm_norm_g[0]), w_out[0].astype(BF16))
    h = _ffn(h, 0, _row(norm_g[0, 2]), _row(norm_g[0, 3]), wg, wu, wd)

    pool = (_row(norm_g[1, 0]), _row(norm_g[1, 1]), pool_w[0].astype(BF16), pool_b[0],
            _row(pool_scale[0]))
    h = _ffn(h, 1, _row(norm_g[1, 2]), _row(norm_g[1, 3]), wg, wu, wd, pool=pool)
    return h.reshape(bsz, seq, D_MODEL)
```

```python
import functools
import math

import jax
import jax.numpy as jnp
from jax import lax
from jax.experimental import pallas as pl
from jax.experimental.pallas import tpu as pltpu

F32 = jnp.float32
BF16 = jnp.bfloat16

EPS = 1e-6
D_MODEL = 1024
CHUNK = 128
GM_HEADS = 4
GM_HEAD_DIM = D_MODEL // GM_HEADS
SSM_HEADS = 16
SSM_HEAD_DIM = 64
SSM_GROUPS = 4
SSM_HEADS_PER_GROUP = SSM_HEADS // SSM_GROUPS
SSM_GROUP_DIM = SSM_HEADS_PER_GROUP * SSM_HEAD_DIM
SSM_STATE = 128
SSM_CONV = 4
POOL_WINDOWS = (2, 4, 8, 16)
POOL_GROUP_DIM = D_MODEL // len(POOL_WINDOWS)
D_FF = 2816

V7X_LANES = 128
V7X_SUBLANES = 8
V7X_MXU_DIM = 256
V7X_VMEM_BYTES = 64 * 1024 * 1024

COL_V = D_MODEL
COL_Z = 2 * D_MODEL
COL_XBC = 3 * D_MODEL
COL_DT = 5 * D_MODEL
XBC_DIM = 2 * D_MODEL
FF_BLOCK = V7X_MXU_DIM
CONV_PAD = V7X_SUBLANES
POOL_PAD = CHUNK


def _tile_plan():
    return dict(mixer_tile=256, ffn_tile=512, vmem_limit=56 * 1024 * 1024)


def _dot(a, b):
    return jnp.dot(a, b, preferred_element_type=F32)


def _rms(x, g):
    return x * lax.rsqrt(jnp.mean(x * x, axis=-1, keepdims=True) + EPS) * g


def _sigmoid(x):
    return 0.5 * (1.0 + jnp.tanh(0.5 * x))


def _silu(x):
    return x * _sigmoid(x)


def _gelu_tanh(x):
    c = math.sqrt(2.0 / math.pi)
    return x * (0.5 * (1.0 + jnp.tanh(c * (x + 0.044715 * (x * x * x)))))


def _softplus(x):
    return jnp.maximum(x, 0.0) + jnp.log1p(jnp.exp(-jnp.abs(x)))


def _split3_bf16(x):
    hi = x.astype(BF16)
    r1 = x - hi.astype(F32)
    mid = r1.astype(BF16)
    lo = (r1 - mid.astype(F32)).astype(BF16)
    return hi, mid, lo


def _ssd_chunk(xs, bm, cm, dt, a_row, dskip, state_ref):
    row = lax.broadcasted_iota(jnp.int32, (CHUNK, CHUNK), 0)
    col = lax.broadcasted_iota(jnp.int32, (CHUNK, CHUNK), 1)
    causal = row >= col
    tril = jnp.where(causal, 1.0, 0.0).astype(BF16)
    hi, mid, lo = _split3_bf16(dt * a_row)
    a_cum = _dot(tril, hi) + _dot(tril, mid) + _dot(tril, lo)
    a_cum_t = a_cum.T
    dt_t = dt.T
    w_t = dt_t * jnp.exp(a_cum_t[:, CHUNK - 1:CHUNK] - a_cum_t)
    a_sub_t = a_cum_t - jnp.log(dt_t)
    head_of_lane = lax.broadcasted_iota(jnp.int32, (1, SSM_GROUP_DIM), 1) // SSM_HEAD_DIM

    b_gs = [bm[:, g * SSM_STATE:(g + 1) * SSM_STATE] for g in range(SSM_GROUPS)]
    c_gs = [cm[:, g * SSM_STATE:(g + 1) * SSM_STATE] for g in range(SSM_GROUPS)]
    cbs = [lax.dot_general(c_g.astype(BF16), b_g.astype(BF16), (((1,), (1,)), ((), ())),
                           preferred_element_type=F32) for b_g, c_g in zip(b_gs, c_gs)]
    ys = []
    for g in range(SSM_GROUPS):
        c_g, cb = c_gs[g], cbs[g]
        xs_g = xs[:, g * SSM_GROUP_DIM:(g + 1) * SSM_GROUP_DIM]
        b_t = b_gs[g].T
        state = state_ref[g]
        xs_bf, state_bf = xs_g.astype(BF16), state.astype(BF16)
        m_parts, ec_parts, bw_parts, xs_parts, st_parts = [], [], [], [], []
        cd_row = jnp.zeros((1, SSM_GROUP_DIM), F32)
        for r in range(SSM_HEADS_PER_GROUP):
            h = g * SSM_HEADS_PER_GROUP + r
            bc = jnp.broadcast_to(a_cum[:, h:h + 1], (CHUNK, CHUNK))
            decay_dt = jnp.exp(jnp.where(causal, bc - a_sub_t[h:h + 1, :], -jnp.inf))
            m_parts.append((cb * decay_dt).astype(BF16))
            ec_parts.append((c_g * jnp.exp(bc)).astype(BF16))
            bw_parts.append((b_t * w_t[h:h + 1, :]).astype(BF16))
            in_head = head_of_lane == r
            xs_parts.append(jnp.where(in_head, xs_bf, 0.0).astype(BF16))
            st_parts.append(jnp.where(in_head, state_bf, 0.0).astype(BF16))
            cd_row = cd_row + jnp.where(in_head, a_cum[CHUNK - 1:CHUNK, h:h + 1], 0.0)
        xs_stack = jnp.concatenate(xs_parts, axis=0)
        lhs_y = jnp.concatenate(m_parts + ec_parts, axis=1)
        rhs_y = jnp.concatenate([xs_stack] + st_parts, axis=0)
        y_g = _dot(lhs_y, rhs_y)
        new = _dot(jnp.concatenate(bw_parts, axis=1), xs_stack)
        state_ref[g] = state * jnp.exp(cd_row) + new
        ys.append(y_g)
    return jnp.concatenate(ys, axis=1) + dskip * xs


def _mixer0_kernel(x_ref, xprev_ref, gpre_ref, gpost_ref, win_ref, wdt_ref, lng_ref, lnb_ref,
                   ws_ref, bmap_ref, convw_ref, convb_ref, dtb_ref, alog_ref, dskip_ref, sng_ref,
                   wout_ref, o_ref, xbc_buf, state_ref, mix_buf, *, tile):
    @pl.when(pl.program_id(0) == 0)
    def _():
        xbc_buf[0:CONV_PAD, :] = jnp.zeros((CONV_PAD, XBC_DIM), F32)
        state_ref[...] = jnp.zeros(state_ref.shape, F32)
        mix_buf[...] = jnp.zeros(mix_buf.shape, BF16)

    x = x_ref[...]
    xn = _rms(x, gpre_ref[...]).astype(BF16)

    xbc_buf[CONV_PAD:CONV_PAD + tile, :] = _dot(xn, win_ref[:, COL_XBC:COL_DT])
    out = _dot(mix_buf[...], wout_ref[...])
    o_ref[...] = xprev_ref[...] + _rms(out, gpost_ref[...])
    z_gate = _silu(_dot(xn, win_ref[:, COL_Z:COL_XBC]))
    ext = xbc_buf[...]
    conv = convb_ref[...] + convw_ref[SSM_CONV - 1:SSM_CONV, :] * ext[CONV_PAD:, :]
    for k in range(SSM_CONV - 1):
        shifted = pltpu.roll(ext, SSM_CONV - 1 - k, axis=0)
        conv = conv + convw_ref[k:k + 1, :] * shifted[CONV_PAD:, :]
    xbc_buf[0:CONV_PAD, :] = ext[tile:tile + CONV_PAD, :]
    xbc = _silu(conv)

    gu = _gelu_tanh(_dot(xn, win_ref[:, 0:COL_V]))
    gv = _gelu_tanh(_dot(xn, win_ref[:, COL_V:COL_Z]))
    dt = _softplus(_dot(xn, wdt_ref[...]) + dtb_ref[...])
    a_row = -jnp.exp(alog_ref[...])
    mu = jnp.mean(gv, axis=-1, keepdims=True)
    gc = gv - mu
    vln = (gc * lax.rsqrt(jnp.mean(gc * gc, axis=-1, keepdims=True) + EPS) * lng_ref[...]
           + lnb_ref[...]).astype(BF16)
    row = lax.broadcasted_iota(jnp.int32, (CHUNK, CHUNK), 0)
    col = lax.broadcasted_iota(jnp.int32, (CHUNK, CHUNK), 1)
    w_causal = [jnp.where(row >= col, ws_ref[h], 0.0).astype(BF16) for h in range(GM_HEADS)]
    for c in range(tile // CHUNK):
        rows = slice(c * CHUNK, (c + 1) * CHUNK)
        y = _ssd_chunk(xbc[rows, 0:D_MODEL], xbc[rows, D_MODEL:D_MODEL + SSM_GROUPS * SSM_STATE],
                       xbc[rows, D_MODEL + SSM_GROUPS * SSM_STATE:XBC_DIM], dt[rows], a_row,
                       dskip_ref[...], state_ref)
        gated = y * z_gate[rows]
        parts = []
        for g in range(SSM_GROUPS):
            gg = gated[:, g * SSM_GROUP_DIM:(g + 1) * SSM_GROUP_DIM]
            parts.append(gg * lax.rsqrt(jnp.mean(gg * gg, axis=-1, keepdims=True) + EPS))
        mix_buf[rows, D_MODEL:2 * D_MODEL] = (
            jnp.concatenate(parts, axis=1) * sng_ref[...]).astype(BF16)
        mixed = jnp.concatenate(
            [_dot(w_causal[h], vln[rows, h * GM_HEAD_DIM:(h + 1) * GM_HEAD_DIM])
             for h in range(GM_HEADS)], axis=1)
        mix_buf[rows, 0:D_MODEL] = (gu[rows] * (mixed + bmap_ref[...])).astype(BF16)


def _const_spec(shape):
    nd = len(shape)
    return pl.BlockSpec(shape, lambda i: (0,) * nd, pipeline_mode=pl.Buffered(1))


def _mixer0(x, gpre, gpost, w_in, w_dt, ln_g, ln_b, ws, bmap, conv_w, conv_b, dt_bias, a_log,
            dskip, sng, w_out):
    plan = _tile_plan()
    tile = plan["mixer_tile"]
    seq = x.shape[0]
    consts = (gpre, gpost, w_in, w_dt, ln_g, ln_b, ws, bmap, conv_w, conv_b, dt_bias, a_log,
              dskip, sng, w_out)
    n_tiles = seq // tile
    cur_spec = pl.BlockSpec((tile, D_MODEL), lambda i: (jnp.minimum(i, n_tiles - 1), 0))
    prev_spec = pl.BlockSpec((tile, D_MODEL), lambda i: (jnp.maximum(i - 1, 0), 0))
    return pl.pallas_call(
        functools.partial(_mixer0_kernel, tile=tile),
        grid=(n_tiles + 1,),
        in_specs=[cur_spec, prev_spec] + [_const_spec(c.shape) for c in consts],
        out_specs=prev_spec,
        out_shape=jax.ShapeDtypeStruct(x.shape, F32),
        scratch_shapes=[
            pltpu.VMEM((CONV_PAD + tile, XBC_DIM), F32),
            pltpu.VMEM((SSM_GROUPS, SSM_STATE, SSM_GROUP_DIM), F32),
            pltpu.VMEM((tile, 2 * D_MODEL), BF16),
        ],
        compiler_params=pltpu.CompilerParams(
            dimension_semantics=("arbitrary",), vmem_limit_bytes=plan["vmem_limit"]),
        name="mixer0",
    )(x, x, *consts)


def _ffn_stage(h, gpre_ref, gpost_ref, wg_ref, wu_ref, wd_ref, act_buf):
    hn = _rms(h, gpre_ref[...]).astype(BF16)
    for j in range(D_FF // FF_BLOCK):
        cols = slice(j * FF_BLOCK, (j + 1) * FF_BLOCK)
        act_buf[:, cols] = (_silu(_dot(hn, wg_ref[:, cols])) * _dot(hn, wu_ref[:, cols])).astype(BF16)
    y = _dot(act_buf[...], wd_ref[...])
    return h + _rms(y, gpost_ref[...])


def _ffn_kernel(h_ref, gpre_ref, gpost_ref, wg_ref, wu_ref, wd_ref, o_ref, act_buf):
    o_ref[...] = _ffn_stage(h_ref[...], gpre_ref, gpost_ref, wg_ref, wu_ref, wd_ref, act_buf)


def _pool_ffn_kernel(h_ref, pgpre_ref, pgpost_ref, pw_ref, pb_ref, pscale_ref,
                     gpre_ref, gpost_ref, wg_ref, wu_ref, wd_ref, o_ref, yhi_buf, ymid_buf, d_buf,
                     act_buf, *, tile):
    @pl.when(pl.program_id(0) == 0)
    def _():
        yhi_buf[0:POOL_PAD, :] = jnp.zeros((POOL_PAD, D_MODEL), BF16)
        ymid_buf[0:POOL_PAD, :] = jnp.zeros((POOL_PAD, D_MODEL), BF16)

    h = h_ref[...]
    y = _rms(h, pgpre_ref[...])
    y_hi = y.astype(BF16)
    yhi_buf[POOL_PAD:POOL_PAD + tile, :] = y_hi
    ymid_buf[POOL_PAD:POOL_PAD + tile, :] = (y - y_hi.astype(F32)).astype(BF16)
    lag = (lax.broadcasted_iota(jnp.int32, (CHUNK, 2 * CHUNK), 0) + CHUNK
           - lax.broadcasted_iota(jnp.int32, (CHUNK, 2 * CHUNK), 1))
    bands = [jnp.where((lag >= 0) & (lag < win), 1.0, 0.0).astype(BF16) for win in POOL_WINDOWS]
    for b in range(tile // CHUNK):
        rows = slice(b * CHUNK, (b + 1) * CHUNK)
        window = slice(b * CHUNK, (b + 2) * CHUNK)
        pos = (pl.program_id(0) * tile + b * CHUNK + 1
               + lax.broadcasted_iota(jnp.int32, (CHUNK, 1), 0))
        for g, win in enumerate(POOL_WINDOWS):
            cols = slice(g * POOL_GROUP_DIM, (g + 1) * POOL_GROUP_DIM)
            wsum = _dot(bands[g], yhi_buf[window, cols]) + _dot(bands[g], ymid_buf[window, cols])
            inv_count = 1.0 / jnp.minimum(pos, win).astype(F32)
            d_buf[rows, cols] = (wsum * inv_count - y[rows, cols]).astype(BF16)
    yhi_buf[0:POOL_PAD, :] = yhi_buf[tile:tile + POOL_PAD, :]
    ymid_buf[0:POOL_PAD, :] = ymid_buf[tile:tile + POOL_PAD, :]
    outs = []
    for g in range(len(POOL_WINDOWS)):
        cols = slice(g * POOL_GROUP_DIM, (g + 1) * POOL_GROUP_DIM)
        outs.append(_dot(d_buf[:, cols], pw_ref[g]) + pb_ref[g:g + 1, :])
    mixed = jnp.concatenate(outs, axis=1) * pscale_ref[...]
    h = h + _rms(mixed, pgpost_ref[...])
    o_ref[...] = _ffn_stage(h, gpre_ref, gpost_ref, wg_ref, wu_ref, wd_ref, act_buf)


def _layer_spec(shape, layer):
    nd = len(shape) - 1
    return pl.BlockSpec((None,) + tuple(shape[1:]), lambda i: (layer,) + (0,) * nd,
                        pipeline_mode=pl.Buffered(1))


def _ffn(h, layer, gpre, gpost, wg, wu, wd, pool=None):
    plan = _tile_plan()
    tile = plan["ffn_tile"]
    seq = h.shape[0]
    tok_spec = pl.BlockSpec((tile, D_MODEL), lambda i: (i, 0))
    consts = (gpre, gpost)
    stacked = (wg, wu, wd)
    scratch = [pltpu.VMEM((tile, D_FF), BF16)]
    if pool is None:
        body, name = _ffn_kernel, "ffn"
    else:
        body, name = functools.partial(_pool_ffn_kernel, tile=tile), "pool_ffn"
        consts = tuple(pool) + consts
        scratch = [pltpu.VMEM((POOL_PAD + tile, D_MODEL), BF16),
                   pltpu.VMEM((POOL_PAD + tile, D_MODEL), BF16),
                   pltpu.VMEM((tile, D_MODEL), BF16)] + scratch
    return pl.pallas_call(
        body,
        grid=(seq // tile,),
        in_specs=([tok_spec] + [_const_spec(c.shape) for c in consts]
                  + [_layer_spec(w.shape, layer) for w in stacked]),
        out_specs=tok_spec,
        out_shape=jax.ShapeDtypeStruct(h.shape, F32),
        scratch_shapes=scratch,
        compiler_params=pltpu.CompilerParams(
            dimension_semantics=("arbitrary",), vmem_limit_bytes=plan["vmem_limit"]),
        name=name,
    )(h, *consts, *stacked)


def _row(v):
    return v.reshape(1, -1).astype(F32)


def _pad_lanes(v):
    return jnp.pad(v, [(0, 0)] * (v.ndim - 1) + [(0, V7X_LANES - v.shape[-1])])


def kernel(x, norm_g, w_in, gm_ln_g, gm_ln_b, gm_ws, gm_bs, conv_w, conv_b, dt_bias, a_log, d_skip,
           ssm_norm_g, w_out, pool_w, pool_b, pool_scale, ffn_w_gate, ffn_w_up, ffn_w_down):
    bsz, seq, _ = x.shape
    assert bsz == 1 and norm_g.shape[0] == 2
    h = x.reshape(seq, D_MODEL)
    wg, wu, wd = (w.astype(BF16) for w in (ffn_w_gate, ffn_w_up, ffn_w_down))

    w_in0 = w_in[0]
    bmap = jnp.repeat(gm_bs[0].T, GM_HEAD_DIM, axis=1)
    h = _mixer0(
        h, _row(norm_g[0, 0]), _row(norm_g[0, 1]),
        w_in0.astype(BF16), _pad_lanes(w_in0[:, COL_DT:]).astype(BF16),
        _row(gm_ln_g[0]), _row(gm_ln_b[0]), gm_ws[0], bmap, conv_w[0], _row(conv_b[0]),
        _pad_lanes(_row(dt_bias[0])), _pad_lanes(_row(a_log[0])),
        _row(jnp.repeat(d_skip[0], SSM_HEAD_DIM)), _row(ssm_norm_g[0]), w_out[0].astype(BF16))
    h = _ffn(h, 0, _row(norm_g[0, 2]), _row(norm_g[0, 3]), wg, wu, wd)

    pool = (_row(norm_g[1, 0]), _row(norm_g[1, 1]), pool_w[0].astype(BF16), pool_b[0],
            _row(pool_scale[0]))
    h = _ffn(h, 1, _row(norm_g[1, 2]), _row(norm_g[1, 3]), wg, wu, wd, pool=pool)
    return h.reshape(bsz, seq, D_MODEL)
```
